```python
import math
import jax
import jax.numpy as jnp
from jax import lax
import numpy as np

D_MODEL = 1024
BATCH = 32
SEQ = 2048
DEPTH = 2
DEC_BATCH = 16
DEC_SEQ = 64
PAST_LEN = 1024

CHUNK = 64
N_MIXERS = 2
N_ATTN_LAYERS = (DEPTH + 1) // 2
N_SSM_LAYERS = DEPTH // 2
N_HEADS = 16
N_KV_HEADS = 4
HEAD_DIM = D_MODEL // N_HEADS
Q_PER_KV = N_HEADS // N_KV_HEADS
QKV_DIM = (N_HEADS + 2 * N_KV_HEADS) * HEAD_DIM
ROT_DIM = HEAD_DIM // 4
ROPE_THETA = 500000.0
WINDOW = 128
BAND_CHUNKS = WINDOW // CHUNK
GROUP_SIZE = 16
N_GROUPS = D_MODEL // GROUP_SIZE
STATE_DIM = 64
D_FF = 2816
CONV_WIDTH = 3
NORM_EPS = 1e-6
NEG_INF = -1e30

kernel_name = 'swa_sink_s5_convffn_stream_step'


def rms_norm(x, g):
    x32 = x.astype(jnp.float32)
    y = x32 * lax.rsqrt(jnp.mean(x32 * x32, axis=-1, keepdims=True) + NORM_EPS) * g.astype(jnp.float32)
    return y.astype(x.dtype)


def partial_rope(x, pos):
    half = ROT_DIM // 2
    inv_freq = jnp.power(jnp.float32(ROPE_THETA), -jnp.arange(half, dtype=jnp.float32) * (2.0 / ROT_DIM))
    ang = pos.astype(jnp.float32)[:, None] * inv_freq[None, :]
    cos = jnp.cos(ang)[None, :, None, :]
    sin = jnp.sin(ang)[None, :, None, :]
    xf = x.astype(jnp.float32)
    x1 = xf[..., :half]
    x2 = xf[..., half:ROT_DIM]
    out = jnp.concatenate([x1 * cos - x2 * sin, x2 * cos + x1 * sin, xf[..., ROT_DIM:]], axis=-1)
    return out.astype(x.dtype)


def qkv_project(h, w_qkv, b_qkv, pos):
    b, t, _ = h.shape
    qkv = h @ w_qkv + b_qkv
    nq = N_HEADS * HEAD_DIM
    nk = N_KV_HEADS * HEAD_DIM
    q = qkv[..., :nq].reshape(b, t, N_HEADS, HEAD_DIM)
    k = qkv[..., nq:nq + nk].reshape(b, t, N_KV_HEADS, HEAD_DIM)
    v = qkv[..., nq + nk:].reshape(b, t, N_KV_HEADS, HEAD_DIM)
    return partial_rope(q, pos), partial_rope(k, pos), v


def sink_softmax(s, sink, mask=None):
    if mask is not None:
        s = jnp.where(mask, s, NEG_INF)
    m = jnp.maximum(jnp.max(s, axis=-1, keepdims=True), sink)
    p = jnp.exp(s - m)
    return p / (jnp.sum(p, axis=-1, keepdims=True) + jnp.exp(sink - m))


def attn_prompt(h, w_qkv, b_qkv, sinks, w_o, b_o):
    b, s_len, _ = h.shape
    n_c = s_len // CHUNK
    q, k, v = qkv_project(h, w_qkv, b_qkv, jnp.arange(s_len))
    qb = q.reshape(b, n_c, CHUNK, N_KV_HEADS, Q_PER_KV, HEAD_DIM)
    pad = ((0, 0), (BAND_CHUNKS * CHUNK, 0), (0, 0), (0, 0))
    kp = jnp.pad(k, pad).reshape(b, n_c + BAND_CHUNKS, CHUNK, N_KV_HEADS, HEAD_DIM)
    vp = jnp.pad(v, pad).reshape(b, n_c + BAND_CHUNKS, CHUNK, N_KV_HEADS, HEAD_DIM)
    kb = jnp.concatenate([kp[:, j:j + n_c] for j in range(BAND_CHUNKS + 1)], axis=2)
    vb = jnp.concatenate([vp[:, j:j + n_c] for j in range(BAND_CHUNKS + 1)], axis=2)
    sc = jnp.einsum('bnqhgd,bnkhd->bnhgqk', qb, kb).astype(jnp.float32) * (HEAD_DIM ** -0.5)
    key_pos = (jnp.arange(n_c)[:, None] - BAND_CHUNKS) * CHUNK + jnp.arange((BAND_CHUNKS + 1) * CHUNK)[None, :]
    mask = (key_pos >= 0)[None, :, None, None, None, :]
    sink = sinks.astype(jnp.float32).reshape(N_KV_HEADS, Q_PER_KV)[None, None, :, :, None, None]
    p = sink_softmax(sc, sink, mask)
    o = jnp.einsum('bnhgqk,bnkhd->bnqhgd', p.astype(vb.dtype), vb).reshape(b, s_len, D_MODEL)
    return o @ w_o + b_o, k[:, -WINDOW:], v[:, -WINDOW:]


def attn_sample(h, ck, cv, w_qkv, b_qkv, sinks, w_o, b_o):
    b, t, _ = h.shape
    q, k, v = qkv_project(h, w_qkv, b_qkv, PAST_LEN + jnp.arange(t))
    kk = jnp.concatenate([ck.astype(k.dtype), k], axis=1)
    vv = jnp.concatenate([cv.astype(v.dtype), v], axis=1)
    qg = q.reshape(b, t, N_KV_HEADS, Q_PER_KV, HEAD_DIM)
    sc = jnp.einsum('bqhgd,bkhd->bhgqk', qg, kk).astype(jnp.float32) * (HEAD_DIM ** -0.5)
    sink = sinks.astype(jnp.float32).reshape(N_KV_HEADS, Q_PER_KV)[None, :, :, None, None]
    p = sink_softmax(sc, sink)
    o = jnp.einsum('bhgqk,bkhd->bqhgd', p.astype(vv.dtype), vv).reshape(b, t, D_MODEL)
    w_rows = ck.shape[1]
    return o @ w_o + b_o, kk[:, -w_rows:], vv[:, -w_rows:]


def _affine_combine(e1, e2):
    ar1, ai1, br1, bi1 = e1
    ar2, ai2, br2, bi2 = e2
    return (ar2 * ar1 - ai2 * ai1,
            ar2 * ai1 + ai2 * ar1,
            ar2 * br1 - ai2 * bi1 + br2,
            ar2 * bi1 + ai2 * br1 + bi2)


def s5_discretize(lam_re, lam_im, log_dt, b_re, b_im):
    lr = lam_re.astype(jnp.float32)
    li = lam_im.astype(jnp.float32)
    dt = jnp.exp(log_dt.astype(jnp.float32))[:, None]
    mag = jnp.exp(lr * dt)
    lbar_re = mag * jnp.cos(li * dt)
    lbar_im = mag * jnp.sin(li * dt)
    nr = lbar_re - 1.0
    ni = lbar_im
    den = lr * lr + li * li
    cr = (nr * lr + ni * li) / den
    ci = (ni * lr - nr * li) / den
    br = b_re.astype(jnp.float32)
    bi = b_im.astype(jnp.float32)
    bbar_re = cr[..., None] * br - ci[..., None] * bi
    bbar_im = cr[..., None] * bi + ci[..., None] * br
    return lbar_re, lbar_im, bbar_re, bbar_im


def s5_mixer(h, h0_re, h0_im, lam_re, lam_im, log_dt, b_re, b_im, c_re, c_im, d_skip, w_glu, b_glu):
    b, t, _ = h.shape
    blk = min(t, CHUNK)
    n_blk = t // blk
    u = h.astype(jnp.float32).reshape(b, n_blk, blk, N_GROUPS, GROUP_SIZE).transpose(1, 0, 2, 3, 4)
    lr, li, bbr, bbi = s5_discretize(lam_re, lam_im, log_dt, b_re, b_im)
    cr = c_re.astype(jnp.float32)
    ci = c_im.astype(jnp.float32)

    def block(carry, u_blk):
        hr, hi = carry
        xr = jnp.einsum('blgi,gpi->blgp', u_blk, bbr)
        xi = jnp.einsum('blgi,gpi->blgp', u_blk, bbi)
        xr = xr.at[:, 0].add(lr * hr - li * hi)
        xi = xi.at[:, 0].add(lr * hi + li * hr)
        ar = jnp.broadcast_to(lr, xr.shape)
        ai = jnp.broadcast_to(li, xi.shape)
        _, _, sr, si = lax.associative_scan(_affine_combine, (ar, ai, xr, xi), axis=1)
        y = jnp.einsum('blgp,gip->blgi', sr, cr) - jnp.einsum('blgp,gip->blgi', si, ci)
        return (sr[:, -1], si[:, -1]), y

    (hr, hi), ys = lax.scan(block, (h0_re.astype(jnp.float32), h0_im.astype(jnp.float32)), u)
    y = ys.transpose(1, 0, 2, 3, 4).reshape(b, t, D_MODEL) + d_skip.astype(jnp.float32) * h.astype(jnp.float32)
    z = jax.nn.gelu(y).astype(h.dtype)
    ag = z @ w_glu + b_glu
    out = ag[..., :D_MODEL] * jax.nn.sigmoid(ag[..., D_MODEL:])
    return out, hr, hi


def conv_ffn(h, prev, w_up, conv_w, conv_b, w_down):
    t = h.shape[1]
    up = h @ w_up
    full = jnp.concatenate([prev.astype(up.dtype), up], axis=1)
    c = conv_b
    for j in range(CONV_WIDTH):
        c = c + conv_w[j] * full[:, j:j + t]
    y = (jax.nn.gelu(c[..., :D_FF]) * c[..., D_FF:]) @ w_down
    return y, full[:, -(CONV_WIDTH - 1):]


def setup_inputs(seed: int = 0) -> dict:
    key = jax.random.key(seed)
    ks = iter(jax.random.split(key, 40))
    f32 = jnp.float32

    def nrm(shape, scale=1.0):
        return jax.random.normal(next(ks), shape, f32) * scale

    na, ns = N_ATTN_LAYERS, N_SSM_LAYERS
    win_rows = min(WINDOW, PAST_LEN)
    lam_im = jnp.pi * jnp.arange(STATE_DIM, dtype=f32)[None, None, :] + nrm((ns, N_GROUPS, STATE_DIM), 0.01)
    return {
        'x_prompt': nrm((BATCH, SEQ, D_MODEL)),
        'x_sample': nrm((DEC_BATCH, DEC_SEQ, D_MODEL)),
        'cache_k': nrm((na, DEC_BATCH, win_rows, N_KV_HEADS, HEAD_DIM)),
        'cache_v': nrm((na, DEC_BATCH, win_rows, N_KV_HEADS, HEAD_DIM)),
        'state_ssm_re': nrm((ns, DEC_BATCH, N_GROUPS, STATE_DIM), 0.5),
        'state_ssm_im': nrm((ns, DEC_BATCH, N_GROUPS, STATE_DIM), 0.5),
        'cache_conv': nrm((DEPTH, DEC_BATCH, CONV_WIDTH - 1, 2 * D_FF)),
        'g_pre_mix': 1.0 + nrm((DEPTH, D_MODEL), 0.05),
        'g_post_mix': 1.0 + nrm((DEPTH, D_MODEL), 0.05),
        'g_pre_ffn': 1.0 + nrm((DEPTH, D_MODEL), 0.05),
        'g_post_ffn': 1.0 + nrm((DEPTH, D_MODEL), 0.05),
        'w_qkv': nrm((na, D_MODEL, QKV_DIM), D_MODEL ** -0.5),
        'b_qkv': nrm((na, QKV_DIM), 0.02),
        'attn_sinks': nrm((na, N_HEADS), 0.5),
        'w_o': nrm((na, D_MODEL, D_MODEL), D_MODEL ** -0.5),
        'b_o': nrm((na, D_MODEL), 0.02),
        'ssm_lam_re': -0.5 + nrm((ns, N_GROUPS, STATE_DIM), 0.01),
        'ssm_lam_im': lam_im,
        'ssm_log_dt': jax.random.uniform(next(ks), (ns, N_GROUPS), f32, math.log(1e-3), math.log(1e-1)),
        'ssm_b_re': nrm((ns, N_GROUPS, STATE_DIM, GROUP_SIZE), (2 * GROUP_SIZE) ** -0.5),
        'ssm_b_im': nrm((ns, N_GROUPS, STATE_DIM, GROUP_SIZE), (2 * GROUP_SIZE) ** -0.5),
        'ssm_c_re': nrm((ns, N_GROUPS, GROUP_SIZE, STATE_DIM), (2 * STATE_DIM) ** -0.5),
        'ssm_c_im': nrm((ns, N_GROUPS, GROUP_SIZE, STATE_DIM), (2 * STATE_DIM) ** -0.5),
        'ssm_d': nrm((ns, D_MODEL)),
        'w_glu': nrm((ns, D_MODEL, 2 * D_MODEL), D_MODEL ** -0.5),
        'b_glu': nrm((ns, 2 * D_MODEL), 0.02),
        'w_up': nrm((DEPTH, D_MODEL, 2 * D_FF), D_MODEL ** -0.5),
        'conv_w': nrm((DEPTH, CONV_WIDTH, 2 * D_FF), CONV_WIDTH ** -0.5),
        'conv_b': nrm((DEPTH, 2 * D_FF), 0.02),
        'w_down': nrm((DEPTH, D_FF, D_MODEL), D_FF ** -0.5),
    }


def reference(x_prompt, x_sample, cache_k, cache_v, state_ssm_re, state_ssm_im, cache_conv,
              g_pre_mix, g_post_mix, g_pre_ffn, g_post_ffn,
              w_qkv, b_qkv, attn_sinks, w_o, b_o,
              ssm_lam_re, ssm_lam_im, ssm_log_dt, ssm_b_re, ssm_b_im, ssm_c_re, ssm_c_im, ssm_d, w_glu, b_glu,
              w_up, conv_w, conv_b, w_down):
    xp, xs = x_prompt, x_sample
    bp = xp.shape[0]
    k_p, v_p, k_s, v_s = [], [], [], []
    re_p, im_p, re_s, im_s = [], [], [], []
    conv_p, conv_s = [], []
    for i in range(DEPTH):
        j = i // N_MIXERS
        hp = rms_norm(xp, g_pre_mix[i])
        hs = rms_norm(xs, g_pre_mix[i])
        if i % N_MIXERS == 0:
            mp, kpi, vpi = attn_prompt(hp, w_qkv[j], b_qkv[j], attn_sinks[j], w_o[j], b_o[j])
            ms, ksi, vsi = attn_sample(hs, cache_k[j], cache_v[j], w_qkv[j], b_qkv[j], attn_sinks[j], w_o[j], b_o[j])
            k_p.append(kpi)
            v_p.append(vpi)
            k_s.append(ksi)
            v_s.append(vsi)
        else:
            ssm_params = (ssm_lam_re[j], ssm_lam_im[j], ssm_log_dt[j], ssm_b_re[j], ssm_b_im[j],
                          ssm_c_re[j], ssm_c_im[j], ssm_d[j], w_glu[j], b_glu[j])
            h0 = jnp.zeros((bp, N_GROUPS, STATE_DIM), jnp.float32)
            mp, hrp, hip = s5_mixer(hp, h0, h0, *ssm_params)
            ms, hrs, his = s5_mixer(hs, state_ssm_re[j], state_ssm_im[j], *ssm_params)
            re_p.append(hrp)
            im_p.append(hip)
            re_s.append(hrs)
            im_s.append(his)
        xp = xp + rms_norm(mp, g_post_mix[i])
        xs = xs + rms_norm(ms, g_post_mix[i])
        hp = rms_norm(xp, g_pre_ffn[i])
        hs = rms_norm(xs, g_pre_ffn[i])
        fp, cpi = conv_ffn(hp, jnp.zeros((bp, CONV_WIDTH - 1, 2 * D_FF), hp.dtype), w_up[i], conv_w[i], conv_b[i], w_down[i])
        fs, csi = conv_ffn(hs, cache_conv[i], w_up[i], conv_w[i], conv_b[i], w_down[i])
        conv_p.append(cpi)
        conv_s.append(csi)
        xp = xp + rms_norm(fp, g_post_ffn[i])
        xs = xs + rms_norm(fs, g_post_ffn[i])
    return (xp, xs,
            jnp.stack(k_p), jnp.stack(v_p), jnp.stack(k_s), jnp.stack(v_s),
            jnp.stack(re_p), jnp.stack(im_p), jnp.stack(re_s), jnp.stack(im_s),
            jnp.stack(conv_p), jnp.stack(conv_s))
```

```python
import functools
import math

import numpy as np
import jax
import jax.numpy as jnp
from jax import lax
from jax.experimental import pallas as pl
from jax.experimental.pallas import tpu as pltpu

F32 = jnp.float32
BF16 = jnp.bfloat16

D_MODEL = 1024
CHUNK = 64
N_HEADS = 16
N_KV_HEADS = 4
HEAD_DIM = 64
Q_PER_KV = N_HEADS // N_KV_HEADS
Q_DIM = N_HEADS * HEAD_DIM
KV_DIM = N_KV_HEADS * HEAD_DIM
QKV_DIM = Q_DIM + 2 * KV_DIM
ROT_DIM = 16
ROT_HALF = ROT_DIM // 2
ROPE_THETA = 500000.0
WINDOW = 128
BAND_CHUNKS = WINDOW // CHUNK
BAND_KEYS = (BAND_CHUNKS + 1) * CHUNK
SAMPLE_POS0 = 1024
GROUP_SIZE = 16
N_GROUPS = 64
STATE_DIM = 64
N_STATES = N_GROUPS * STATE_DIM
D_FF = 2816
CONV_WIDTH = 3
NORM_EPS = 1e-6
NEG_INF = -1e30

LANES = 128
SUBLANES = 8
MXU_DIM = 256
VMEM_LIMIT_BYTES = 56 * 1024 * 1024

SSM_BATCH = SUBLANES
SSM_GROUPS_PER_SLICE = MXU_DIM // GROUP_SIZE
SSM_SLICES = D_MODEL // MXU_DIM
SSM_SLICE_STATES = SSM_GROUPS_PER_SLICE * STATE_DIM
SSM_SLABS = 2 * SSM_SLICE_STATES // LANES
SSM_PITCH_PAD = SUBLANES


def _rms(x, g):
    ms = jnp.mean(x * x, axis=-1, keepdims=True)
    return x * lax.rsqrt(ms + NORM_EPS) * g


def _gelu(x):
    c = math.sqrt(2.0 / math.pi)
    return 0.5 * x * (1.0 + jnp.tanh(c * (x + 0.044715 * (x * x * x))))


def _const_spec(shape):
    nd = len(shape)
    return pl.BlockSpec(shape, lambda *_: (0,) * nd)


def _params(n_axes):
    return pltpu.CompilerParams(dimension_semantics=("arbitrary",) * n_axes,
                                vmem_limit_bytes=VMEM_LIMIT_BYTES)


def _rope_table_kernel(o_ref, *, pos0):
    t = o_ref.shape[1]
    pos = (lax.broadcasted_iota(jnp.int32, (t, LANES), 0) + pos0).astype(F32)
    d = lax.broadcasted_iota(jnp.int32, (t, LANES), 1) % HEAD_DIM
    i = d % ROT_HALF
    inv_freq = jnp.zeros((t, LANES), F32)
    for k in range(ROT_HALF):
        inv_freq = jnp.where(i == k, np.float32(ROPE_THETA ** (-k * 2.0 / ROT_DIM)), inv_freq)
    ang = pos * inv_freq
    c = jnp.cos(ang)
    s = jnp.sin(ang)
    o_ref[0] = jnp.where(d < ROT_DIM, c, 1.0)
    o_ref[1] = jnp.where(d < ROT_HALF, -s, 0.0)
    o_ref[2] = jnp.where((d >= ROT_HALF) & (d < ROT_DIM), s, 0.0)


def _rope_table(t, pos0):
    return pl.pallas_call(
        functools.partial(_rope_table_kernel, pos0=pos0),
        out_shape=jax.ShapeDtypeStruct((3, t, LANES), F32),
        name="rope_table",
    )()


def _qkv_kernel(x_ref, g_ref, w_ref, b_ref, tab_ref, q_ref, k_ref, v_ref):
    h = _rms(x_ref[0], g_ref[...]).astype(BF16)
    qkv = jnp.dot(h, w_ref[...], preferred_element_type=F32) + b_ref[...]
    c, s_lo, s_hi = tab_ref[0], tab_ref[1], tab_ref[2]
    scale = HEAD_DIM ** -0.5
    for j in range((Q_DIM + KV_DIM) // LANES):
        blk = qkv[:, j * LANES:(j + 1) * LANES]
        rot = (blk * c + pltpu.roll(blk, LANES - ROT_HALF, axis=1) * s_lo
               + pltpu.roll(blk, ROT_HALF, axis=1) * s_hi)
        if j < Q_DIM // LANES:
            q_ref[0, :, j * LANES:(j + 1) * LANES] = (rot * scale).astype(BF16)
        else:
            jj = j - Q_DIM // LANES
            k_ref[0, :, jj * LANES:(jj + 1) * LANES] = rot
    v_ref[0] = qkv[:, Q_DIM + KV_DIM:]


def _qkv_rope(x, g, w_bf, b, tab, ts):
    bsz, s, _ = x.shape
    return pl.pallas_call(
        _qkv_kernel,
        grid=(bsz, s // ts),
        in_specs=[
            pl.BlockSpec((1, ts, D_MODEL), lambda b_, s_: (b_, s_, 0)),
            _const_spec((1, D_MODEL)),
            _const_spec((D_MODEL, QKV_DIM)),
            _const_spec((1, QKV_DIM)),
            pl.BlockSpec((3, ts, LANES), lambda b_, s_: (0, s_, 0)),
        ],
        out_specs=[
            pl.BlockSpec((1, ts, Q_DIM), lambda b_, s_: (b_, s_, 0)),
            pl.BlockSpec((1, ts, KV_DIM), lambda b_, s_: (b_, s_, 0)),
            pl.BlockSpec((1, ts, KV_DIM), lambda b_, s_: (b_, s_, 0)),
        ],
        out_shape=[
            jax.ShapeDtypeStruct((bsz, s, Q_DIM), BF16),
            jax.ShapeDtypeStruct((bsz, s, KV_DIM), F32),
            jax.ShapeDtypeStruct((bsz, s, KV_DIM), F32),
        ],
        compiler_params=_params(2),
        name="qkv_rope",
    )(x, g, w_bf, b, tab)


def _attn_kernel(sink_ref, q_ref, k_ref, v_ref, x_ref, wo_ref, bo_ref, g_ref, o_ref, attn_scr,
                 *, tq, key_chunk_offset):
    qt = pl.program_id(1)
    rows = Q_PER_KV * CHUNK
    row = lax.broadcasted_iota(jnp.int32, (rows, 1), 0)
    kidx = lax.broadcasted_iota(jnp.int32, (1, BAND_KEYS), 1)
    sink_cols = []
    for g in range(N_KV_HEADS):
        col = jnp.full((rows, 1), sink_ref[g * Q_PER_KV], F32)
        for j in range(1, Q_PER_KV):
            col = jnp.where(row >= j * CHUNK, sink_ref[g * Q_PER_KV + j], col)
        sink_cols.append(col)

    def chunk_body(c, carry):
        r0 = pl.multiple_of(c * CHUNK, CHUNK)
        kc = qt * (tq // CHUNK) + c + key_chunk_offset
        start = pl.multiple_of(jnp.maximum(kc - BAND_CHUNKS, 0) * CHUNK, CHUNK)
        valid = kidx < (kc + 1) * CHUNK - start
        for g in range(N_KV_HEADS):
            heads = [g * Q_PER_KV + j for j in range(Q_PER_KV)]
            qg = jnp.concatenate(
                [q_ref[0, pl.ds(r0, CHUNK), h * HEAD_DIM:(h + 1) * HEAD_DIM] for h in heads], axis=0)
            kw = k_ref[0, pl.ds(start, BAND_KEYS), g * HEAD_DIM:(g + 1) * HEAD_DIM].astype(BF16)
            vw = v_ref[0, pl.ds(start, BAND_KEYS), g * HEAD_DIM:(g + 1) * HEAD_DIM].astype(BF16)
            s = lax.dot_general(qg, kw, (((1,), (1,)), ((), ())), preferred_element_type=F32)
            s = jnp.where(valid, s, NEG_INF)
            sink = sink_cols[g]
            m = jnp.maximum(jnp.max(s, axis=-1, keepdims=True), sink)
            p = jnp.exp(s - m)
            denom = jnp.sum(p, axis=-1, keepdims=True) + jnp.exp(sink - m)
            o = jnp.dot(p.astype(BF16), vw, preferred_element_type=F32) / denom
            for j, h in enumerate(heads):
                attn_scr[pl.ds(r0, CHUNK), h * HEAD_DIM:(h + 1) * HEAD_DIM] = o[j * CHUNK:(j + 1) * CHUNK]
        return carry

    lax.fori_loop(0, tq // CHUNK, chunk_body, 0)
    mixed = jnp.dot(attn_scr[...].astype(BF16), wo_ref[...], preferred_element_type=F32) + bo_ref[...]
    o_ref[0] = x_ref[0] + _rms(mixed, g_ref[...])


def _attention(sinks, q, k, v, x, wo_bf, bo, g_post, tq):
    bsz, s, _ = q.shape
    tk = k.shape[1]
    kernel = functools.partial(_attn_kernel, tq=tq, key_chunk_offset=(tk - s) // CHUNK)
    return pl.pallas_call(
        kernel,
        grid=(bsz, s // tq),
        in_specs=[
            pl.BlockSpec(memory_space=pltpu.SMEM),
            pl.BlockSpec((1, tq, Q_DIM), lambda b_, s_: (b_, s_, 0)),
            pl.BlockSpec((1, tk, KV_DIM), lambda b_, s_: (b_, 0, 0)),
            pl.BlockSpec((1, tk, KV_DIM), lambda b_, s_: (b_, 0, 0)),
            pl.BlockSpec((1, tq, D_MODEL), lambda b_, s_: (b_, s_, 0)),
            _const_spec((D_MODEL, D_MODEL)),
            _const_spec((1, D_MODEL)),
            _const_spec((1, D_MODEL)),
        ],
        out_specs=pl.BlockSpec((1, tq, D_MODEL), lambda b_, s_: (b_, s_, 0)),
        out_shape=jax.ShapeDtypeStruct((bsz, s, D_MODEL), F32),
        scratch_shapes=[pltpu.VMEM((tq, D_MODEL), F32)],
        compiler_params=_params(2),
        name="attention",
    )(sinks, q, k, v, x, wo_bf, bo, g_post)


def _ffn_kernel(x_ref, prev_ref, gpre_ref, wup_ref, cw_ref, cb_ref, wdn_ref, gpost_ref,
                o_ref, cache_ref, carry_scr, *, ts, fc):
    @pl.when(pl.program_id(1) == 0)
    def _():
        carry_scr[...] = prev_ref[0]

    x = x_ref[0]
    h = _rms(x, gpre_ref[...]).astype(BF16)
    row = lax.broadcasted_iota(jnp.int32, (ts, 1), 0)
    acc = jnp.zeros((ts, D_MODEL), F32)
    for j in range(D_FF // fc):
        halves = []
        for half in range(2):
            c0 = half * D_FF + j * fc
            up = jnp.dot(h, wup_ref[:, c0:c0 + fc], preferred_element_type=F32)
            p0 = carry_scr[0:1, c0:c0 + fc]
            p1 = carry_scr[1:2, c0:c0 + fc]
            m1 = jnp.where(row == 0, p1, pltpu.roll(up, 1, axis=0))
            m2 = jnp.where(row == 0, p0, jnp.where(row == 1, p1, pltpu.roll(up, 2, axis=0)))
            carry_scr[:, c0:c0 + fc] = up[ts - 2:ts]
            halves.append(cb_ref[:, c0:c0 + fc] + cw_ref[0:1, c0:c0 + fc] * m2
                          + cw_ref[1:2, c0:c0 + fc] * m1 + cw_ref[2:3, c0:c0 + fc] * up)
        act = (_gelu(halves[0]) * halves[1]).astype(BF16)
        acc = acc + jnp.dot(act, wdn_ref[j * fc:(j + 1) * fc, :], preferred_element_type=F32)
    o_ref[0] = x + _rms(acc, gpost_ref[...])
    cache_ref[0] = carry_scr[...]


def _conv_ffn(x, prev, g_pre, wup_bf, conv_w, conv_b, wdn_bf, g_post, ts, fc):
    bsz, s, _ = x.shape
    kernel = functools.partial(_ffn_kernel, ts=ts, fc=fc)
    return pl.pallas_call(
        kernel,
        grid=(bsz, s // ts),
        in_specs=[
            pl.BlockSpec((1, ts, D_MODEL), lambda b_, s_: (b_, s_, 0)),
            pl.BlockSpec((1, CONV_WIDTH - 1, 2 * D_FF), lambda b_, s_: (b_, 0, 0)),
            _const_spec((1, D_MODEL)),
            _const_spec((D_MODEL, 2 * D_FF)),
            _const_spec((CONV_WIDTH, 2 * D_FF)),
            _const_spec((1, 2 * D_FF)),
            _const_spec((D_FF, D_MODEL)),
            _const_spec((1, D_MODEL)),
        ],
        out_specs=[
            pl.BlockSpec((1, ts, D_MODEL), lambda b_, s_: (b_, s_, 0)),
            pl.BlockSpec((1, CONV_WIDTH - 1, 2 * D_FF), lambda b_, s_: (b_, 0, 0)),
        ],
        out_shape=[
            jax.ShapeDtypeStruct((bsz, s, D_MODEL), F32),
            jax.ShapeDtypeStruct((bsz, CONV_WIDTH - 1, 2 * D_FF), F32),
        ],
        scratch_shapes=[pltpu.VMEM((CONV_WIDTH - 1, 2 * D_FF), F32)],
        compiler_params=_params(2),
        name="conv_ffn",
    )(x, prev, g_pre, wup_bf, conv_w, conv_b, wdn_bf, g_post)


def _discretize_kernel(lr_ref, li_ref, logdt_ref, br_ref, bi_ref, lbr_ref, lbi_ref, bbr_ref, bbi_ref):
    lr = lr_ref[...]
    li = li_ref[...]
    dt = jnp.exp(logdt_ref[...])
    mag = jnp.exp(lr * dt)
    lbar_re = mag * jnp.cos(li * dt)
    lbar_im = mag * jnp.sin(li * dt)
    nr = lbar_re - 1.0
    ni = lbar_im
    den = lr * lr + li * li
    cr = (nr * lr + ni * li) / den
    ci = (ni * lr - nr * li) / den
    br = br_ref[...]
    bi = bi_ref[...]
    lbr_ref[...] = lbar_re
    lbi_ref[...] = lbar_im
    bbr_ref[...] = cr * br - ci * bi
    bbi_ref[...] = cr * bi + ci * br


def _discretize(lam_re, lam_im, log_dt, b_re, b_im):
    rep = lambda a: jnp.repeat(a, GROUP_SIZE, axis=0)
    bt = lambda a: a.transpose(0, 2, 1).reshape(N_GROUPS * GROUP_SIZE, STATE_DIM)
    shape = jax.ShapeDtypeStruct((N_GROUPS * GROUP_SIZE, STATE_DIM), F32)
    lbr, lbi, bbr, bbi = pl.pallas_call(
        _discretize_kernel, out_shape=[shape] * 4, name="s5_discretize",
    )(rep(lam_re), rep(lam_im), rep(jnp.broadcast_to(log_dt[:, None], (N_GROUPS, STATE_DIM))),
      bt(b_re), bt(b_im))
    first = lambda a: a.reshape(N_GROUPS, GROUP_SIZE, STATE_DIM)[:, 0]
    to_gjp = lambda a: a.reshape(N_GROUPS, GROUP_SIZE, STATE_DIM)
    return first(lbr), first(lbi), to_gjp(bbr), to_gjp(bbi)


def _block_diag(a):
    n = SSM_GROUPS_PER_SLICE
    eye = jnp.eye(n, dtype=jnp.bool_)[None, :, None, :, None]
    out = jnp.where(eye, a[:, :, :, None, :], jnp.zeros((), a.dtype))
    return out.reshape(SSM_SLICES, n * a.shape[2], n * a.shape[3])


def _ssm_matrices(bbr, bbi, c_re, c_im):
    n = SSM_GROUPS_PER_SLICE
    slc = lambda a: a.reshape(SSM_SLICES, n, *a.shape[1:])
    b_in = jnp.concatenate([_block_diag(slc(bbr)), _block_diag(slc(bbi))], axis=2)
    cr = slc(c_re.transpose(0, 2, 1))
    ci = slc(-c_im.transpose(0, 2, 1))
    c_out = jnp.concatenate([_block_diag(cr), _block_diag(ci)], axis=1)
    return b_in.astype(BF16), c_out.astype(BF16)


def _ssm_kernel(x_ref, h0r_ref, h0i_ref, gpre_ref, lbr_ref, lbi_ref, bin_ref, cout_ref, d_ref,
                wglu_ref, bglu_ref, gpost_ref, o_ref, hr_ref, hi_ref, scr, *, tt):
    @pl.when(pl.program_id(1) == 0)
    def _():
        hr_ref[...] = h0r_ref[...]
        hi_ref[...] = h0i_ref[...]

    nb = SSM_BATCH
    pitch = tt + SSM_PITCH_PAD
    half = SSM_SLABS // 2
    x = x_ref[...].reshape(nb * tt, D_MODEL)
    u = _rms(x, gpre_ref[...])
    ub = u.astype(BF16)
    ys = []
    for q in range(SSM_SLICES):
        xs = jnp.dot(ub[:, q * MXU_DIM:(q + 1) * MXU_DIM], bin_ref[q], preferred_element_type=F32)
        for slab in range(SSM_SLABS):
            for b in range(nb):
                scr[slab, b * pitch:b * pitch + tt, :] = xs[b * tt:(b + 1) * tt, slab * LANES:(slab + 1) * LANES]
        lo = q * SSM_SLICE_STATES
        lam_r = [jnp.broadcast_to(lbr_ref[:, lo + s * LANES:lo + (s + 1) * LANES], (nb, LANES)) for s in range(half)]
        lam_i = [jnp.broadcast_to(lbi_ref[:, lo + s * LANES:lo + (s + 1) * LANES], (nb, LANES)) for s in range(half)]
        h_r = tuple(hr_ref[:, lo + s * LANES:lo + (s + 1) * LANES] for s in range(half))
        h_i = tuple(hi_ref[:, lo + s * LANES:lo + (s + 1) * LANES] for s in range(half))

        def step(t, carry):
            h_r, h_i = carry
            new_r, new_i = [], []
            for s in range(half):
                xr = scr[s, pl.ds(t, nb, stride=pitch), :]
                xi = scr[half + s, pl.ds(t, nb, stride=pitch), :]
                nr = lam_r[s] * h_r[s] - lam_i[s] * h_i[s] + xr
                ni = lam_r[s] * h_i[s] + lam_i[s] * h_r[s] + xi
                scr[s, pl.ds(t, nb, stride=pitch), :] = nr
                scr[half + s, pl.ds(t, nb, stride=pitch), :] = ni
                new_r.append(nr)
                new_i.append(ni)
            return tuple(new_r), tuple(new_i)

        h_r, h_i = lax.fori_loop(0, tt, step, (h_r, h_i))
        for s in range(half):
            hr_ref[:, lo + s * LANES:lo + (s + 1) * LANES] = h_r[s]
            hi_ref[:, lo + s * LANES:lo + (s + 1) * LANES] = h_i[s]
        hb = jnp.concatenate(
            [jnp.concatenate([scr[slab, b * pitch:b * pitch + tt, :] for b in range(nb)], axis=0)
             for slab in range(SSM_SLABS)], axis=1).astype(BF16)
        ys.append(jnp.dot(hb, cout_ref[q], preferred_element_type=F32))
    y = jnp.concatenate(ys, axis=1) + d_ref[...] * u
    z = _gelu(y).astype(BF16)
    ag = jnp.dot(z, wglu_ref[...], preferred_element_type=F32) + bglu_ref[...]
    gate = ag[:, D_MODEL:]
    out = ag[:, :D_MODEL] * (1.0 / (1.0 + jnp.exp(-gate)))
    o_ref[...] = (x + _rms(out, gpost_ref[...])).reshape(nb, tt, D_MODEL)


def _ssm_mixer(x, h0r, h0i, g_pre, lbr, lbi, b_in, c_out, d_skip, wglu_bf, bglu, g_post, tt):
    bsz, s, _ = x.shape
    nb = SSM_BATCH
    pitch = tt + SSM_PITCH_PAD
    state_spec = pl.BlockSpec((nb, N_STATES), lambda b_, s_: (b_, 0))
    kernel = functools.partial(_ssm_kernel, tt=tt)
    return pl.pallas_call(
        kernel,
        grid=(bsz // nb, s // tt),
        in_specs=[
            pl.BlockSpec((nb, tt, D_MODEL), lambda b_, s_: (b_, s_, 0)),
            state_spec,
            state_spec,
            _const_spec((1, D_MODEL)),
            _const_spec((1, N_STATES)),
            _const_spec((1, N_STATES)),
            _const_spec((SSM_SLICES, MXU_DIM, 2 * SSM_SLICE_STATES)),
            _const_spec((SSM_SLICES, 2 * SSM_SLICE_STATES, MXU_DIM)),
            _const_spec((1, D_MODEL)),
            _const_spec((D_MODEL, 2 * D_MODEL)),
            _const_spec((1, 2 * D_MODEL)),
            _const_spec((1, D_MODEL)),
        ],
        out_specs=[
            pl.BlockSpec((nb, tt, D_MODEL), lambda b_, s_: (b_, s_, 0)),
            state_spec,
            state_spec,
        ],
        out_shape=[
            jax.ShapeDtypeStruct((bsz, s, D_MODEL), F32),
            jax.ShapeDtypeStruct((bsz, N_STATES), F32),
            jax.ShapeDtypeStruct((bsz, N_STATES), F32),
        ],
        scratch_shapes=[pltpu.VMEM((SSM_SLABS, nb * pitch, LANES), F32)],
        compiler_params=_params(2),
        name="s5_mixer",
    )(x, h0r, h0i, g_pre, lbr, lbi, b_in, c_out, d_skip, wglu_bf, bglu, g_post)


def _row(a):
    return a.reshape(1, -1)


def kernel(x_prompt, x_sample, cache_k, cache_v, state_ssm_re, state_ssm_im, cache_conv, g_pre_mix, g_post_mix, g_pre_ffn, g_post_ffn, w_qkv, b_qkv, attn_sinks, w_o, b_o, ssm_lam_re, ssm_lam_im, ssm_log_dt, ssm_b_re, ssm_b_im, ssm_c_re, ssm_c_im, ssm_d, w_glu, b_glu, w_up, conv_w, conv_b, w_down):
    bp, sp, _ = x_prompt.shape
    bs, ss, _ = x_sample.shape
    win_rows = cache_k.shape[2]
    ts_p = min(sp, 512)
    tq_p = min(sp, 256)

    wqkv_bf = w_qkv[0].astype(BF16)
    wo_bf = w_o[0].astype(BF16)
    args0 = (_row(g_pre_mix[0]), wqkv_bf, _row(b_qkv[0]))
    qp, kp, vp = _qkv_rope(x_prompt, *args0, _rope_table(sp, 0), ts_p)
    qs, ks, vs = _qkv_rope(x_sample, *args0, _rope_table(ss, SAMPLE_POS0), ss)
    kk = jnp.concatenate([cache_k[0].reshape(bs, win_rows, KV_DIM), ks], axis=1)
    vv = jnp.concatenate([cache_v[0].reshape(bs, win_rows, KV_DIM), vs], axis=1)
    attn_args = (wo_bf, _row(b_o[0]), _row(g_post_mix[0]))
    xp = _attention(attn_sinks[0], qp, kp, vp, x_prompt, *attn_args, tq_p)
    xs = _attention(attn_sinks[0], qs, kk, vv, x_sample, *attn_args, ss)
    kv_shape = lambda a, b: a.reshape(1, b, -1, N_KV_HEADS, HEAD_DIM)
    k_p = kv_shape(kp[:, sp - WINDOW:], bp)
    v_p = kv_shape(vp[:, sp - WINDOW:], bp)
    k_s = kv_shape(kk[:, kk.shape[1] - win_rows:], bs)
    v_s = kv_shape(vv[:, vv.shape[1] - win_rows:], bs)

    conv_p, conv_s = [], []

    def ffn_both(i, xp, xs):
        args = (_row(g_pre_ffn[i]), w_up[i].astype(BF16), conv_w[i], _row(conv_b[i]),
                w_down[i].astype(BF16), _row(g_post_ffn[i]))
        zero_prev = jnp.zeros((bp, CONV_WIDTH - 1, 2 * D_FF), F32)
        xp, cp = _conv_ffn(xp, zero_prev, *args, ts_p, 256)
        xs, cs = _conv_ffn(xs, cache_conv[i], *args, ss, 256)
        conv_p.append(cp)
        conv_s.append(cs)
        return xp, xs

    xp, xs = ffn_both(0, xp, xs)

    lbr, lbi, bbr, bbi = _discretize(ssm_lam_re[0], ssm_lam_im[0], ssm_log_dt[0], ssm_b_re[0], ssm_b_im[0])
    b_in, c_out = _ssm_matrices(bbr, bbi, ssm_c_re[0], ssm_c_im[0])
    ssm_args = (_row(g_pre_mix[1]), _row(lbr), _row(lbi), b_in, c_out, _row(ssm_d[0]),
                w_glu[0].astype(BF16), _row(b_glu[0]), _row(g_post_mix[1]))
    zero_state = jnp.zeros((bp, N_STATES), F32)
    xp, hrp, hip = _ssm_mixer(xp, zero_state, zero_state, *ssm_args, CHUNK)
    xs, hrs, his = _ssm_mixer(xs, state_ssm_re[0].reshape(bs, N_STATES), state_ssm_im[0].reshape(bs, N_STATES),
                              *ssm_args, CHUNK)
    st = lambda a, b: a.reshape(1, b, N_GROUPS, STATE_DIM)

    xp, xs = ffn_both(1, xp, xs)

    return (xp, xs, k_p, v_p, k_s, v_s,
            st(hrp, bp), st(hip, bp), st(hrs, bs), st(his, bs),
            jnp.stack(conv_p), jnp.stack(conv_s))
```

```python
import functools
import math

import numpy as np
import jax
import jax.numpy as jnp
from jax import lax
from jax.experimental import pallas as pl
from jax.experimental.pallas import tpu as pltpu

F32 = jnp.float32
BF16 = jnp.bfloat16

D_MODEL = 1024
CHUNK = 64
N_HEADS = 16
N_KV_HEADS = 4
HEAD_DIM = 64
Q_PER_KV = N_HEADS // N_KV_HEADS
Q_DIM = N_HEADS * HEAD_DIM
KV_DIM = N_KV_HEADS * HEAD_DIM
QKV_DIM = Q_DIM + 2 * KV_DIM
ROT_DIM = 16
ROT_HALF = ROT_DIM // 2
ROPE_THETA = 500000.0
WINDOW = 128
BAND_CHUNKS = WINDOW // CHUNK
BAND_KEYS = (BAND_CHUNKS + 1) * CHUNK
SAMPLE_POS0 = 1024
GROUP_SIZE = 16
N_GROUPS = 64
STATE_DIM = 64
N_STATES = N_GROUPS * STATE_DIM
D_FF = 2816
CONV_WIDTH = 3
NORM_EPS = 1e-6
NEG_INF = -1e30

LANES = 128
SUBLANES = 8
MXU_DIM = 256
VMEM_LIMIT_BYTES = 56 * 1024 * 1024

SSM_BATCH = SUBLANES
SSM_GROUPS_PER_SLICE = MXU_DIM // GROUP_SIZE
SSM_SLICES = D_MODEL // MXU_DIM
SSM_SLICE_STATES = SSM_GROUPS_PER_SLICE * STATE_DIM
SSM_SLABS = 2 * SSM_SLICE_STATES // LANES
SSM_PITCH_PAD = 4

QKV_TILE_PROMPT = 512
ATTN_TILE_PROMPT = 512
FFN_TILES_PROMPT = (1024, 512, 256)


def _rms(x, g):
    ms = jnp.mean(x * x, axis=-1, keepdims=True)
    return x * lax.rsqrt(ms + NORM_EPS) * g


def _gelu(x):
    c = math.sqrt(2.0 / math.pi)
    half_x = 0.5 * x
    return half_x + half_x * jnp.tanh(x * (c + (c * 0.044715) * (x * x)))


def _const_spec(shape):
    nd = len(shape)
    return pl.BlockSpec(shape, lambda *_: (0,) * nd)


def _resident_spec(shape):
    nd = len(shape)
    return pl.BlockSpec(shape, lambda *_: (0,) * nd, pipeline_mode=pl.Buffered(1))


def _params(n_axes):
    return pltpu.CompilerParams(dimension_semantics=("arbitrary",) * n_axes,
                                vmem_limit_bytes=VMEM_LIMIT_BYTES)


def _rope_table_kernel(o_ref, *, pos0):
    t = o_ref.shape[1]
    pos = (lax.broadcasted_iota(jnp.int32, (t, LANES), 0) + pos0).astype(F32)
    d = lax.broadcasted_iota(jnp.int32, (t, LANES), 1) % HEAD_DIM
    i = d % ROT_HALF
    inv_freq = jnp.zeros((t, LANES), F32)
    for k in range(ROT_HALF):
        inv_freq = jnp.where(i == k, np.float32(ROPE_THETA ** (-k * 2.0 / ROT_DIM)), inv_freq)
    ang = pos * inv_freq
    c = jnp.cos(ang)
    s = jnp.sin(ang)
    o_ref[0] = jnp.where(d < ROT_DIM, c, 1.0)
    o_ref[1] = jnp.where(d < ROT_HALF, -s, 0.0)
    o_ref[2] = jnp.where((d >= ROT_HALF) & (d < ROT_DIM), s, 0.0)


def _rope_table(t, pos0):
    return pl.pallas_call(
        functools.partial(_rope_table_kernel, pos0=pos0),
        out_shape=jax.ShapeDtypeStruct((3, t, LANES), F32),
        name="rope_table",
    )()


def _qkv_kernel(x_ref, g_ref, w_ref, b_ref, tab_ref, q_ref, k_ref, v_ref):
    h = _rms(x_ref[0], g_ref[...]).astype(BF16)
    qkv = jnp.dot(h, w_ref[...], preferred_element_type=F32) + b_ref[...]
    c, s_lo, s_hi = tab_ref[0], tab_ref[1], tab_ref[2]
    scale = HEAD_DIM ** -0.5
    for j in range((Q_DIM + KV_DIM) // LANES):
        blk = qkv[:, j * LANES:(j + 1) * LANES]
        rot = (blk * c + pltpu.roll(blk, LANES - ROT_HALF, axis=1) * s_lo
               + pltpu.roll(blk, ROT_HALF, axis=1) * s_hi)
        if j < Q_DIM // LANES:
            q_ref[0, :, j * LANES:(j + 1) * LANES] = (rot * scale).astype(BF16)
        else:
            jj = j - Q_DIM // LANES
            k_ref[0, :, jj * LANES:(jj + 1) * LANES] = rot
    v_ref[0] = qkv[:, Q_DIM + KV_DIM:]


def _qkv_rope(x, g, w_bf, b, tab, ts):
    bsz, s, _ = x.shape
    return pl.pallas_call(
        _qkv_kernel,
        grid=(bsz, s // ts),
        in_specs=[
            pl.BlockSpec((1, ts, D_MODEL), lambda b_, s_: (b_, s_, 0)),
            _const_spec((1, D_MODEL)),
            _const_spec((D_MODEL, QKV_DIM)),
            _const_spec((1, QKV_DIM)),
            pl.BlockSpec((3, ts, LANES), lambda b_, s_: (0, s_, 0)),
        ],
        out_specs=[
            pl.BlockSpec((1, ts, Q_DIM), lambda b_, s_: (b_, s_, 0)),
            pl.BlockSpec((1, ts, KV_DIM), lambda b_, s_: (b_, s_, 0)),
            pl.BlockSpec((1, ts, KV_DIM), lambda b_, s_: (b_, s_, 0)),
        ],
        out_shape=[
            jax.ShapeDtypeStruct((bsz, s, Q_DIM), BF16),
            jax.ShapeDtypeStruct((bsz, s, KV_DIM), F32),
            jax.ShapeDtypeStruct((bsz, s, KV_DIM), F32),
        ],
        compiler_params=_params(2),
        name="qkv_rope",
    )(x, g, w_bf, b, tab)


def _to_attn_head_order(a, axis):
    shape = a.shape
    a = a.reshape(*shape[:axis], N_KV_HEADS // 2, 2, Q_PER_KV, HEAD_DIM, *shape[axis + 1:])
    return jnp.swapaxes(a, axis + 1, axis + 2).reshape(shape)


def _attn_kernel(sink_ref, q_ref, k_ref, v_ref, x_ref, wo_ref, bo_ref, g_ref, o_ref, attn_scr,
                 *, tq, key_chunk_offset):
    qt = pl.program_id(1)
    rows = Q_PER_KV * CHUNK
    row = lax.broadcasted_iota(jnp.int32, (rows, 1), 0)
    kidx = lax.broadcasted_iota(jnp.int32, (1, BAND_KEYS), 1)
    sink_cols = []
    for g in range(N_KV_HEADS):
        col = jnp.full((rows, 1), sink_ref[g * Q_PER_KV], F32)
        for j in range(1, Q_PER_KV):
            col = jnp.where(row >= j * CHUNK, sink_ref[g * Q_PER_KV + j], col)
        sink_cols.append(col)

    low_half = lax.broadcasted_iota(jnp.int32, (1, LANES), 1) < HEAD_DIM
    ones = jnp.ones((BAND_KEYS, LANES), BF16)

    def chunk_body(c, carry):
        r0 = c * CHUNK
        kc = qt * (tq // CHUNK) + c + key_chunk_offset
        start = pl.multiple_of(jnp.maximum(kc - BAND_CHUNKS, 0) * CHUNK, CHUNK)
        valid = kidx < (kc + 1) * CHUNK - start
        for pair in range(N_KV_HEADS // 2):
            pair_lanes = slice(pair * LANES, (pair + 1) * LANES)
            q_blk = jnp.concatenate(
                [q_ref[0, pl.ds(r0, CHUNK), (Q_PER_KV * pair + j) * LANES:(Q_PER_KV * pair + j + 1) * LANES]
                 for j in range(Q_PER_KV)], axis=0)
            k_pair = k_ref[0, pl.ds(start, BAND_KEYS), pair_lanes]
            v_pair = v_ref[0, pl.ds(start, BAND_KEYS), pair_lanes].astype(BF16)
            v_and_ones = jnp.concatenate([v_pair, ones], axis=1)
            outs = []
            for side in range(2):
                keep = low_half if side == 0 else jnp.logical_not(low_half)
                k_side = jnp.where(keep, k_pair, 0.0).astype(BF16)
                s = lax.dot_general(q_blk, k_side, (((1,), (1,)), ((), ())), preferred_element_type=F32)
                s = jnp.where(valid, s, NEG_INF)
                sink = sink_cols[2 * pair + side]
                m = jnp.maximum(jnp.max(s, axis=-1, keepdims=True), sink)
                p = jnp.exp(s - m).astype(BF16)
                pv = jnp.dot(p, v_and_ones, preferred_element_type=F32)
                outs.append(pv[:, :LANES] / (pv[:, LANES:] + jnp.exp(sink - m)))
            merged = jnp.where(low_half, outs[0], outs[1])
            for j in range(Q_PER_KV):
                blk = Q_PER_KV * pair + j
                attn_scr[pl.ds(r0, CHUNK), blk * LANES:(blk + 1) * LANES] = merged[j * CHUNK:(j + 1) * CHUNK]
        return carry

    for c in range(tq // CHUNK):
        chunk_body(c, 0)
    mixed = jnp.dot(attn_scr[...].astype(BF16), wo_ref[...], preferred_element_type=F32) + bo_ref[...]
    o_ref[0] = x_ref[0] + _rms(mixed, g_ref[...])


def _attention(sinks, q, k, v, x, wo_bf, bo, g_post, tq):
    bsz, s, _ = q.shape
    tk = k.shape[1]
    kernel = functools.partial(_attn_kernel, tq=tq, key_chunk_offset=(tk - s) // CHUNK)
    return pl.pallas_call(
        kernel,
        grid=(bsz, s // tq),
        in_specs=[
            pl.BlockSpec(memory_space=pltpu.SMEM),
            pl.BlockSpec((1, tq, Q_DIM), lambda b_, s_: (b_, s_, 0)),
            pl.BlockSpec((1, tk, KV_DIM), lambda b_, s_: (b_, 0, 0)),
            pl.BlockSpec((1, tk, KV_DIM), lambda b_, s_: (b_, 0, 0)),
            pl.BlockSpec((1, tq, D_MODEL), lambda b_, s_: (b_, s_, 0)),
            _const_spec((D_MODEL, D_MODEL)),
            _const_spec((1, D_MODEL)),
            _const_spec((1, D_MODEL)),
        ],
        out_specs=pl.BlockSpec((1, tq, D_MODEL), lambda b_, s_: (b_, s_, 0)),
        out_shape=jax.ShapeDtypeStruct((bsz, s, D_MODEL), F32),
        scratch_shapes=[pltpu.VMEM((tq, D_MODEL), F32)],
        compiler_params=_params(2),
        name="attention",
    )(sinks, q, k, v, x, wo_bf, bo, g_post)


def _ffn_kernel(x_ref, prev_ref, gpre_ref, wup_ref, cw_ref, cb_ref, wdn_ref, gpost_ref,
                o_ref, cache_ref, carry_scr, up_scr, *, ts, sub, fc):
    @pl.when(pl.program_id(1) == 0)
    def _():
        carry_scr[...] = prev_ref[0]

    halo = SUBLANES
    taps = CONV_WIDTH - 1
    slabs_per_half = fc // LANES
    n_chunks = D_FF // fc
    items = [(t, j) for t in range(ts // sub) for j in range(n_chunks)]
    normed = {}

    def rows(t):
        return slice(t * sub, (t + 1) * sub)

    def project_up(k):
        t, j = items[k]
        if t not in normed:
            normed[t] = _rms(x_ref[0, rows(t), :], gpre_ref[...]).astype(BF16)
        for half in range(2):
            c0 = half * D_FF + j * fc
            up = jnp.dot(normed[t], wup_ref[:, c0:c0 + fc], preferred_element_type=F32)
            for s in range(slabs_per_half):
                lanes = slice(c0 + s * LANES, c0 + (s + 1) * LANES)
                slab = half * slabs_per_half + s
                up_scr[k % 2, slab, halo - taps:halo, :] = carry_scr[:, lanes]
                up_scr[k % 2, slab, halo:halo + sub, :] = up[:, s * LANES:(s + 1) * LANES]
            carry_scr[:, c0:c0 + fc] = up[sub - taps:sub]

    def conv_gate(k):
        _, j = items[k]
        halves = []
        for half in range(2):
            cols = []
            for s in range(slabs_per_half):
                c0 = half * D_FF + j * fc + s * LANES
                slab = half * slabs_per_half + s
                c = cb_ref[:, c0:c0 + LANES]
                for tap in range(CONV_WIDTH):
                    r0 = halo - taps + tap
                    c = c + cw_ref[tap:tap + 1, c0:c0 + LANES] * up_scr[k % 2, slab, r0:r0 + sub, :]
                cols.append(c)
            halves.append(cols)
        return jnp.concatenate([(_gelu(a) * b).astype(BF16) for a, b in zip(*halves)], axis=1)

    project_up(0)
    acc = None
    for k, (t, j) in enumerate(items):
        if k + 1 < len(items):
            project_up(k + 1)
        down = jnp.dot(conv_gate(k), wdn_ref[j * fc:(j + 1) * fc, :], preferred_element_type=F32)
        acc = down if j == 0 else acc + down
        if j == n_chunks - 1:
            o_ref[0, rows(t), :] = x_ref[0, rows(t), :] + _rms(acc, gpost_ref[...])
    cache_ref[0] = carry_scr[...]


def _conv_ffn(x, prev, g_pre, wup_bf, conv_w, conv_b, wdn_bf, g_post, ts, sub, fc):
    bsz, s, _ = x.shape
    kernel = functools.partial(_ffn_kernel, ts=ts, sub=sub, fc=fc)
    return pl.pallas_call(
        kernel,
        grid=(bsz, s // ts),
        in_specs=[
            pl.BlockSpec((1, ts, D_MODEL), lambda b_, s_: (b_, s_, 0)),
            pl.BlockSpec((1, CONV_WIDTH - 1, 2 * D_FF), lambda b_, s_: (b_, 0, 0)),
            _const_spec((1, D_MODEL)),
            _resident_spec((D_MODEL, 2 * D_FF)),
            _const_spec((CONV_WIDTH, 2 * D_FF)),
            _const_spec((1, 2 * D_FF)),
            _resident_spec((D_FF, D_MODEL)),
            _const_spec((1, D_MODEL)),
        ],
        out_specs=[
            pl.BlockSpec((1, ts, D_MODEL), lambda b_, s_: (b_, s_, 0)),
            pl.BlockSpec((1, CONV_WIDTH - 1, 2 * D_FF), lambda b_, s_: (b_, 0, 0)),
        ],
        out_shape=[
            jax.ShapeDtypeStruct((bsz, s, D_MODEL), F32),
            jax.ShapeDtypeStruct((bsz, CONV_WIDTH - 1, 2 * D_FF), F32),
        ],
        scratch_shapes=[pltpu.VMEM((CONV_WIDTH - 1, 2 * D_FF), F32),
                        pltpu.VMEM((2, 2 * fc // LANES, SUBLANES + sub, LANES), F32)],
        compiler_params=_params(2),
        name="conv_ffn",
    )(x, prev, g_pre, wup_bf, conv_w, conv_b, wdn_bf, g_post)


def _discretize_kernel(lr_ref, li_ref, logdt_ref, br_ref, bi_ref, lbr_ref, lbi_ref, bbr_ref, bbi_ref):
    lr = lr_ref[...]
    li = li_ref[...]
    dt = jnp.exp(logdt_ref[...])
    mag = jnp.exp(lr * dt)
    lbar_re = mag * jnp.cos(li * dt)
    lbar_im = mag * jnp.sin(li * dt)
    nr = lbar_re - 1.0
    ni = lbar_im
    den = lr * lr + li * li
    cr = (nr * lr + ni * li) / den
    ci = (ni * lr - nr * li) / den
    br = br_ref[...]
    bi = bi_ref[...]
    lbr_ref[...] = lbar_re
    lbi_ref[...] = lbar_im
    bbr_ref[...] = cr * br - ci * bi
    bbi_ref[...] = cr * bi + ci * br


def _discretize(lam_re, lam_im, log_dt, b_re, b_im):
    rep = lambda a: jnp.repeat(a, GROUP_SIZE, axis=0)
    bt = lambda a: a.transpose(0, 2, 1).reshape(N_GROUPS * GROUP_SIZE, STATE_DIM)
    shape = jax.ShapeDtypeStruct((N_GROUPS * GROUP_SIZE, STATE_DIM), F32)
    lbr, lbi, bbr, bbi = pl.pallas_call(
        _discretize_kernel, out_shape=[shape] * 4, name="s5_discretize",
    )(rep(lam_re), rep(lam_im), rep(jnp.broadcast_to(log_dt[:, None], (N_GROUPS, STATE_DIM))),
      bt(b_re), bt(b_im))
    first = lambda a: a.reshape(N_GROUPS, GROUP_SIZE, STATE_DIM)[:, 0]
    to_gjp = lambda a: a.reshape(N_GROUPS, GROUP_SIZE, STATE_DIM)
    return first(lbr), first(lbi), to_gjp(bbr), to_gjp(bbi)


def _block_diag(a):
    n = SSM_GROUPS_PER_SLICE
    eye = jnp.eye(n, dtype=jnp.bool_)[None, :, None, :, None]
    out = jnp.where(eye, a[:, :, :, None, :], jnp.zeros((), a.dtype))
    return out.reshape(SSM_SLICES, n * a.shape[2], n * a.shape[3])


def _ssm_matrices(bbr, bbi, c_re, c_im):
    n = SSM_GROUPS_PER_SLICE
    slc = lambda a: a.reshape(SSM_SLICES, n, *a.shape[1:])
    b_in = jnp.concatenate([_block_diag(slc(bbr)), _block_diag(slc(bbi))], axis=2)
    cr = slc(c_re.transpose(0, 2, 1))
    ci = slc(-c_im.transpose(0, 2, 1))
    c_out = jnp.concatenate([_block_diag(cr), _block_diag(ci)], axis=1)
    return b_in.astype(BF16), c_out.astype(BF16)


def _ssm_kernel(x_ref, h0r_ref, h0i_ref, gpre_ref, lbr_ref, lbi_ref, bin_ref, cout_ref, d_ref,
                wglu_ref, bglu_ref, gpost_ref, o_ref, hr_ref, hi_ref, scr, *, tt):
    @pl.when(pl.program_id(1) == 0)
    def _():
        hr_ref[...] = h0r_ref[...]
        hi_ref[...] = h0i_ref[...]

    nb = SSM_BATCH
    pitch = tt + SSM_PITCH_PAD
    half = SSM_SLABS // 2
    x = x_ref[...].reshape(nb * tt, D_MODEL)
    u = _rms(x, gpre_ref[...])
    ub = u.astype(BF16)
    def project_in(q):
        xs = jnp.dot(ub[:, q * MXU_DIM:(q + 1) * MXU_DIM], bin_ref[q], preferred_element_type=F32)
        for slab in range(SSM_SLABS):
            for b in range(nb):
                scr[q % 2, slab, b * pitch:b * pitch + tt, :] = xs[b * tt:(b + 1) * tt, slab * LANES:(slab + 1) * LANES]

    def scan(q):
        lo = q * SSM_SLICE_STATES
        lanes = [slice(lo + s * LANES, lo + (s + 1) * LANES) for s in range(half)]
        lam_r = [jnp.broadcast_to(lbr_ref[:, l], (nb, LANES)) for l in lanes]
        lam_i = [jnp.broadcast_to(lbi_ref[:, l], (nb, LANES)) for l in lanes]
        h_r = [hr_ref[:, l] for l in lanes]
        h_i = [hi_ref[:, l] for l in lanes]
        for t in range(tt):
            at_t = pl.ds(t, nb, stride=pitch)
            for s in range(half):
                nr = lam_r[s] * h_r[s] - lam_i[s] * h_i[s] + scr[q % 2, s, at_t, :]
                ni = lam_r[s] * h_i[s] + lam_i[s] * h_r[s] + scr[q % 2, half + s, at_t, :]
                scr[q % 2, s, at_t, :] = nr
                scr[q % 2, half + s, at_t, :] = ni
                h_r[s], h_i[s] = nr, ni
        for s, l in enumerate(lanes):
            hr_ref[:, l] = h_r[s]
            hi_ref[:, l] = h_i[s]

    def project_out(q):
        hb = jnp.concatenate(
            [jnp.concatenate([scr[q % 2, slab, b * pitch:b * pitch + tt, :] for b in range(nb)], axis=0)
             for slab in range(SSM_SLABS)], axis=1).astype(BF16)
        return jnp.dot(hb, cout_ref[q], preferred_element_type=F32)

    ys = []
    project_in(0)
    for q in range(SSM_SLICES):
        if q + 1 < SSM_SLICES:
            project_in(q + 1)
        scan(q)
        ys.append(project_out(q))
    y = jnp.concatenate(ys, axis=1) + d_ref[...] * u
    z = _gelu(y).astype(BF16)
    ag = jnp.dot(z, wglu_ref[...], preferred_element_type=F32) + bglu_ref[...]
    gate = ag[:, D_MODEL:]
    out = ag[:, :D_MODEL] * (1.0 / (1.0 + jnp.exp(-gate)))
    o_ref[...] = (x + _rms(out, gpost_ref[...])).reshape(nb, tt, D_MODEL)


def _ssm_mixer(x, h0r, h0i, g_pre, lbr, lbi, b_in, c_out, d_skip, wglu_bf, bglu, g_post, tt):
    bsz, s, _ = x.shape
    nb = SSM_BATCH
    pitch = tt + SSM_PITCH_PAD
    state_spec = pl.BlockSpec((nb, N_STATES), lambda b_, s_: (b_, 0))
    kernel = functools.partial(_ssm_kernel, tt=tt)
    return pl.pallas_call(
        kernel,
        grid=(bsz // nb, s // tt),
        in_specs=[
            pl.BlockSpec((nb, tt, D_MODEL), lambda b_, s_: (b_, s_, 0)),
            state_spec,
            state_spec,
            _const_spec((1, D_MODEL)),
            _const_spec((1, N_STATES)),
            _const_spec((1, N_STATES)),
            _resident_spec((SSM_SLICES, MXU_DIM, 2 * SSM_SLICE_STATES)),
            _resident_spec((SSM_SLICES, 2 * SSM_SLICE_STATES, MXU_DIM)),
            _const_spec((1, D_MODEL)),
            _resident_spec((D_MODEL, 2 * D_MODEL)),
            _const_spec((1, 2 * D_MODEL)),
            _const_spec((1, D_MODEL)),
        ],
        out_specs=[
            pl.BlockSpec((nb, tt, D_MODEL), lambda b_, s_: (b_, s_, 0)),
            state_spec,
            state_spec,
        ],
        out_shape=[
            jax.ShapeDtypeStruct((bsz, s, D_MODEL), F32),
            jax.ShapeDtypeStruct((bsz, N_STATES), F32),
            jax.ShapeDtypeStruct((bsz, N_STATES), F32),
        ],
        scratch_shapes=[pltpu.VMEM((2, SSM_SLABS, nb * pitch, LANES), F32)],
        compiler_params=_params(2),
        name="s5_mixer",
    )(x, h0r, h0i, g_pre, lbr, lbi, b_in, c_out, d_skip, wglu_bf, bglu, g_post)


def _row(a):
    return a.reshape(1, -1)


def kernel(x_prompt, x_sample, cache_k, cache_v, state_ssm_re, state_ssm_im, cache_conv, g_pre_mix, g_post_mix, g_pre_ffn, g_post_ffn, w_qkv, b_qkv, attn_sinks, w_o, b_o, ssm_lam_re, ssm_lam_im, ssm_log_dt, ssm_b_re, ssm_b_im, ssm_c_re, ssm_c_im, ssm_d, w_glu, b_glu, w_up, conv_w, conv_b, w_down):
    bp, sp, _ = x_prompt.shape
    bs, ss, _ = x_sample.shape
    win_rows = cache_k.shape[2]
    ts_p = min(sp, QKV_TILE_PROMPT)
    tq_p = min(sp, ATTN_TILE_PROMPT)
    ffn_ts, ffn_sub, ffn_fc = FFN_TILES_PROMPT
    ffn_ts, ffn_sub = min(sp, ffn_ts), min(sp, ffn_sub)

    wqkv_bf = jnp.concatenate(
        [_to_attn_head_order(w_qkv[0][:, :Q_DIM], 1), w_qkv[0][:, Q_DIM:]], axis=1).astype(BF16)
    bqkv = jnp.concatenate([_to_attn_head_order(b_qkv[0][:Q_DIM], 0), b_qkv[0][Q_DIM:]])
    wo_bf = _to_attn_head_order(w_o[0], 0).astype(BF16)
    args0 = (_row(g_pre_mix[0]), wqkv_bf, _row(bqkv))
    qp, kp, vp = _qkv_rope(x_prompt, *args0, _rope_table(sp, 0), ts_p)
    qs, ks, vs = _qkv_rope(x_sample, *args0, _rope_table(ss, SAMPLE_POS0), ss)
    kk = jnp.concatenate([cache_k[0].reshape(bs, win_rows, KV_DIM), ks], axis=1)
    vv = jnp.concatenate([cache_v[0].reshape(bs, win_rows, KV_DIM), vs], axis=1)
    attn_args = (wo_bf, _row(b_o[0]), _row(g_post_mix[0]))
    xp = _attention(attn_sinks[0], qp, kp, vp, x_prompt, *attn_args, tq_p)
    xs = _attention(attn_sinks[0], qs, kk, vv, x_sample, *attn_args, ss)
    kv_shape = lambda a, b: a.reshape(1, b, -1, N_KV_HEADS, HEAD_DIM)
    k_p = kv_shape(kp[:, sp - WINDOW:], bp)
    v_p = kv_shape(vp[:, sp - WINDOW:], bp)
    k_s = kv_shape(kk[:, kk.shape[1] - win_rows:], bs)
    v_s = kv_shape(vv[:, vv.shape[1] - win_rows:], bs)

    conv_p, conv_s = [], []

    def ffn_both(i, xp, xs):
        args = (_row(g_pre_ffn[i]), w_up[i].astype(BF16), conv_w[i], _row(conv_b[i]),
                w_down[i].astype(BF16), _row(g_post_ffn[i]))
        zero_prev = jnp.zeros((bp, CONV_WIDTH - 1, 2 * D_FF), F32)
        xp, cp = _conv_ffn(xp, zero_prev, *args, ffn_ts, ffn_sub, ffn_fc)
        xs, cs = _conv_ffn(xs, cache_conv[i], *args, ss, ss, ffn_fc)
        conv_p.append(cp)
        conv_s.append(cs)
        return xp, xs

    xp, xs = ffn_both(0, xp, xs)

    lbr, lbi, bbr, bbi = _discretize(ssm_lam_re[0], ssm_lam_im[0], ssm_log_dt[0], ssm_b_re[0], ssm_b_im[0])
    b_in, c_out = _ssm_matrices(bbr, bbi, ssm_c_re[0], ssm_c_im[0])
    ssm_args = (_row(g_pre_mix[1]), _row(lbr), _row(lbi), b_in, c_out, _row(ssm_d[0]),
                w_glu[0].astype(BF16), _row(b_glu[0]), _row(g_post_mix[1]))
    zero_state = jnp.zeros((bp, N_STATES), F32)
    xp, hrp, hip = _ssm_mixer(xp, zero_state, zero_state, *ssm_args, CHUNK)
    xs, hrs, his = _ssm_mixer(xs, state_ssm_re[0].reshape(bs, N_STATES), state_ssm_im[0].reshape(bs, N_STATES),
                              *ssm_args, CHUNK)
    st = lambda a, b: a.reshape(1, b, N_GROUPS, STATE_DIM)

    xp, xs = ffn_both(1, xp, xs)

    return (xp, xs, k_p, v_p, k_s, v_s,
            st(hrp, bp), st(hip, bp), st(hrs, bs), st(his, bs),
            jnp.stack(conv_p), jnp.stack(conv_s))
```
